```python
import jax, jax.numpy as jnp
from jax import lax
import numpy as np

D_MODEL = 1024
BATCH = 2
SEQ = 8192
DEPTH = 2

CHUNK = 64
N_META = 16
N_PAD = CHUNK - N_META
EPS = 1e-6

ML_HEADS = 8
ML_DV = D_MODEL // ML_HEADS
ML_DK = ML_DV // 2
ML_QK = ML_HEADS * ML_DK
ML_V = ML_HEADS * ML_DV
ML_CONV = 4
GATE_CAP = 15.0

RT_HEADS = D_MODEL // 256
RT_DK = D_MODEL // RT_HEADS
RT_DV = 2 * RT_DK
RT_QK = RT_HEADS * RT_DK
RT_V = RT_HEADS * RT_DV
ROPE_BASE = 10000.0

D_FF = ((8 * D_MODEL + 3 * 256 - 1) // (3 * 256)) * 256

N_MLSTM = (DEPTH + 1) // 2
N_RET = DEPTH // 2

kernel_name = 'hybrid_mlstm_retention_chunk_causal'


def rmsnorm(x, g):
    x32 = x.astype(jnp.float32)
    y = x32 * lax.rsqrt(jnp.mean(x32 * x32, axis=-1, keepdims=True) + EPS)
    return (y * g.astype(jnp.float32)).astype(x.dtype)


def causal_depthwise_conv(u, w):
    K, C = w.shape
    return lax.conv_general_dilated(u, w[:, None, :].astype(u.dtype), window_strides=(1,),
                                    padding=[(K - 1, 0)], dimension_numbers=('NWC', 'WIO', 'NWC'),
                                    feature_group_count=C)


def rope(x, pos):
    half = x.shape[-1] // 2
    inv = ROPE_BASE ** (-jnp.arange(half, dtype=jnp.float32) / half)
    ang = pos.astype(jnp.float32)[:, None] * inv[None, :]
    cos = jnp.cos(ang)[None, :, None, :]
    sin = jnp.sin(ang)[None, :, None, :]
    x1, x2 = x[..., :half], x[..., half:]
    return jnp.concatenate([x1 * cos - x2 * sin, x2 * cos + x1 * sin], axis=-1)


def to_chunks(t):
    B, L, H, d = t.shape
    return t.reshape(B, L // CHUNK, CHUNK, H, d).transpose(1, 0, 3, 2, 4)


def gate_chunks(t):
    B, L, H = t.shape
    return t.reshape(B, L // CHUNK, CHUNK, H).transpose(1, 0, 3, 2)


def from_chunks(t):
    NC, B, H, C, d = t.shape
    return t.transpose(1, 0, 3, 2, 4).reshape(B, NC * C, H, d)


def mlstm_chunk_step(carry, xs):
    C_prev, n_prev, m_prev = carry
    q, k, v, log_i, log_f = xs
    F = jnp.cumsum(log_f, axis=-1)
    D = log_i[..., None, :] - jnp.abs(F[..., :, None] - F[..., None, :])
    inter_log = F + m_prev[..., None]
    m = jnp.maximum(inter_log, jnp.max(D, axis=-1))
    w = jnp.exp(D - m[..., None])
    a = jnp.exp(inter_log - m)
    s = jnp.einsum('bhid,bhjd->bhij', q, k) * w
    num = a[..., None] * jnp.einsum('bhid,bhde->bhie', q, C_prev) + jnp.einsum('bhij,bhje->bhie', s, v)
    den = a * jnp.einsum('bhid,bhd->bhi', q, n_prev) + jnp.sum(s, axis=-1)
    h = num / jnp.maximum(jnp.abs(den), jnp.exp(-m))[..., None]
    F_end = F[..., -1]
    g_log = log_i + F_end[..., None] - F
    m_new = jnp.maximum(F_end + m_prev, jnp.max(g_log, axis=-1))
    g = jnp.exp(g_log - m_new[..., None])
    decay = jnp.exp(F_end + m_prev - m_new)
    C_new = decay[..., None, None] * C_prev + jnp.einsum('bhj,bhjd,bhje->bhde', g, k, v)
    n_new = decay[..., None] * n_prev + jnp.einsum('bhj,bhjd->bhd', g, k)
    return (C_new, n_new, m_new), h


def mlstm_mixer(h, valid, w_in, b_gate, w_conv, g_norm, w_out):
    B, L, _ = h.shape
    f32 = jnp.float32
    u = h @ w_in
    vmask = valid[None, :, None]
    qk = jnp.where(vmask, u[..., :2 * ML_QK], 0.0)
    v = u[..., 2 * ML_QK:2 * ML_QK + ML_V]
    o = u[..., 2 * ML_QK + ML_V:2 * ML_QK + 2 * ML_V]
    gates = u[..., 2 * ML_QK + 2 * ML_V:].astype(f32) + b_gate.astype(f32)
    qk = jax.nn.silu(causal_depthwise_conv(qk, w_conv)).astype(f32)
    q = qk[..., :ML_QK].reshape(B, L, ML_HEADS, ML_DK) * (ML_DK ** -0.5)
    k = qk[..., ML_QK:].reshape(B, L, ML_HEADS, ML_DK)
    v = v.astype(f32).reshape(B, L, ML_HEADS, ML_DV)
    gates = GATE_CAP * jnp.tanh(gates / GATE_CAP)
    log_i = jnp.where(vmask, gates[..., :ML_HEADS], -jnp.inf)
    log_f = jnp.where(vmask, jax.nn.log_sigmoid(gates[..., ML_HEADS:]), 0.0)
    init = (jnp.zeros((B, ML_HEADS, ML_DK, ML_DV), f32),
            jnp.zeros((B, ML_HEADS, ML_DK), f32),
            jnp.zeros((B, ML_HEADS), f32))
    xs = (to_chunks(q), to_chunks(k), to_chunks(v), gate_chunks(log_i), gate_chunks(log_f))
    _, hs = lax.scan(mlstm_chunk_step, init, xs)
    hs = from_chunks(hs)
    hs = hs * lax.rsqrt(jnp.mean(hs * hs, axis=-1, keepdims=True) + EPS)
    hs = hs.reshape(B, L, ML_V) * g_norm.astype(f32) * jax.nn.sigmoid(o.astype(f32))
    return hs.astype(h.dtype) @ w_out


def retention_mixer(h, valid, pos, w_in, g_norm, w_out):
    B, L, _ = h.shape
    f32 = jnp.float32
    u = h @ w_in
    q = u[..., :RT_QK].astype(f32).reshape(B, L, RT_HEADS, RT_DK)
    k = u[..., RT_QK:2 * RT_QK].astype(f32).reshape(B, L, RT_HEADS, RT_DK) * (RT_DK ** -0.5)
    v = u[..., 2 * RT_QK:2 * RT_QK + RT_V].astype(f32).reshape(B, L, RT_HEADS, RT_DV)
    g = u[..., 2 * RT_QK + RT_V:].astype(f32)
    q = rope(q, pos)
    k = jnp.where(valid[None, :, None, None], rope(k, pos), 0.0)
    log_gamma = jnp.log1p(-jnp.exp2(-5.0 - jnp.arange(RT_HEADS, dtype=f32)))
    idx = jnp.arange(CHUNK, dtype=f32)
    intra_decay = jnp.exp(log_gamma[:, None, None] * jnp.abs(idx[:, None] - idx[None, :]))
    q_decay = jnp.exp(log_gamma[:, None] * (idx + 1.0))[None, :, :, None]
    k_decay = jnp.exp(log_gamma[:, None] * (CHUNK - 1.0 - idx))[None, :, :, None]
    chunk_decay = jnp.exp(log_gamma * CHUNK)[None, :, None, None]

    def step(S, xs):
        qc, kc, vc = xs
        s = jnp.einsum('bhid,bhjd->bhij', qc, kc) * intra_decay
        out = jnp.einsum('bhij,bhje->bhie', s, vc) + q_decay * jnp.einsum('bhid,bhde->bhie', qc, S)
        S = chunk_decay * S + jnp.einsum('bhjd,bhje->bhde', kc * k_decay, vc)
        return S, out

    S0 = jnp.zeros((B, RT_HEADS, RT_DK, RT_DV), f32)
    _, hs = lax.scan(step, S0, (to_chunks(q), to_chunks(k), to_chunks(v)))
    hs = from_chunks(hs)
    mu = jnp.mean(hs, axis=-1, keepdims=True)
    var = jnp.mean((hs - mu) ** 2, axis=-1, keepdims=True)
    hs = ((hs - mu) * lax.rsqrt(var + EPS)).reshape(B, L, RT_V)
    hs = hs * g_norm.astype(f32) * jax.nn.silu(g)
    return hs.astype(h.dtype) @ w_out


def swiglu(h, w_gu, w_down):
    gu = h @ w_gu
    return (jax.nn.silu(gu[..., :D_FF]) * gu[..., D_FF:]) @ w_down


def setup_inputs(seed: int = 0) -> dict:
    key = jax.random.key(seed)
    ks = jax.random.split(key, 16)
    f32 = jnp.float32

    def dense(k, shape, fan_in):
        return jax.random.normal(k, shape, f32) * (fan_in ** -0.5)

    def gain(k, shape):
        return 1.0 + 0.02 * jax.random.normal(k, shape, f32)

    x = jax.random.normal(ks[0], (BATCH, SEQ, D_MODEL), f32)
    meta_tokens = jax.random.normal(ks[1], (N_META, D_MODEL), f32)
    norm_mix = gain(ks[2], (DEPTH, D_MODEL))
    norm_ffn = gain(ks[3], (DEPTH, D_MODEL))
    norm_final = gain(ks[4], (D_MODEL,))
    ml_w_in = dense(ks[5], (N_MLSTM, D_MODEL, 2 * ML_QK + 2 * ML_V + 2 * ML_HEADS), D_MODEL)
    b_i = 0.1 * jax.random.normal(ks[6], (N_MLSTM, ML_HEADS), f32)
    b_f = jnp.linspace(3.0, 6.0, ML_HEADS, dtype=f32)[None, :] + 0.1 * jax.random.normal(ks[7], (N_MLSTM, ML_HEADS), f32)
    ml_b_gate = jnp.concatenate([b_i, b_f], axis=-1)
    ml_w_conv = dense(ks[8], (N_MLSTM, ML_CONV, 2 * ML_QK), ML_CONV)
    ml_norm = gain(ks[9], (N_MLSTM, ML_V))
    ml_w_out = dense(ks[10], (N_MLSTM, ML_V, D_MODEL), ML_V)
    rt_w_in = dense(ks[11], (N_RET, D_MODEL, 2 * RT_QK + 2 * RT_V), D_MODEL)
    rt_norm = gain(ks[12], (N_RET, RT_V))
    rt_w_out = dense(ks[13], (N_RET, RT_V, D_MODEL), RT_V)
    ffn_w_gu = dense(ks[14], (DEPTH, D_MODEL, 2 * D_FF), D_MODEL)
    ffn_w_down = dense(ks[15], (DEPTH, D_FF, D_MODEL), D_FF)
    return {'x': x, 'meta_tokens': meta_tokens, 'norm_mix': norm_mix, 'norm_ffn': norm_ffn,
            'norm_final': norm_final, 'ml_w_in': ml_w_in, 'ml_b_gate': ml_b_gate,
            'ml_w_conv': ml_w_conv, 'ml_norm': ml_norm, 'ml_w_out': ml_w_out,
            'rt_w_in': rt_w_in, 'rt_norm': rt_norm, 'rt_w_out': rt_w_out,
            'ffn_w_gu': ffn_w_gu, 'ffn_w_down': ffn_w_down}


def reference(x, meta_tokens, norm_mix, norm_ffn, norm_final, ml_w_in, ml_b_gate, ml_w_conv,
              ml_norm, ml_w_out, rt_w_in, rt_norm, rt_w_out, ffn_w_gu, ffn_w_down):
    B = x.shape[0]
    meta = jnp.broadcast_to(meta_tokens.astype(x.dtype)[None], (B, N_META, D_MODEL))
    pad = jnp.zeros((B, N_PAD, D_MODEL), x.dtype)
    h = jnp.concatenate([pad, meta, x], axis=1)
    L = h.shape[1]
    valid = jnp.arange(L) >= N_PAD
    pos = jnp.arange(L, dtype=jnp.int32) - N_PAD
    for layer in range(DEPTH):
        j = layer // 2
        hn = rmsnorm(h, norm_mix[layer])
        if layer % 2 == 0:
            h = h + mlstm_mixer(hn, valid, ml_w_in[j], ml_b_gate[j], ml_w_conv[j], ml_norm[j], ml_w_out[j])
        else:
            h = h + retention_mixer(hn, valid, pos, rt_w_in[j], rt_norm[j], rt_w_out[j])
        h = h + swiglu(rmsnorm(h, norm_ffn[layer]), ffn_w_gu[layer], ffn_w_down[layer])
    return rmsnorm(h, norm_final)[:, CHUNK:, :]
```

```python
import functools

import jax
import jax.numpy as jnp
from jax import lax
from jax.experimental import pallas as pl
from jax.experimental.pallas import tpu as pltpu

F32 = jnp.float32
BF16 = jnp.bfloat16

D_MODEL = 1024
CHUNK = 64
CHUNK_SHIFT = 6
N_META = 16
N_PAD = CHUNK - N_META
EPS = 1e-6

ML_HEADS = 8
ML_DV = 128
ML_DK = 64
ML_QK = ML_HEADS * ML_DK
ML_V = ML_HEADS * ML_DV
ML_CONV = 4
GATE_CAP = 15.0
ML_U = 2 * ML_QK + 2 * ML_V

RT_HEADS = 4
RT_DK = 256
RT_DV = 512
RT_QK = RT_HEADS * RT_DK
RT_V = RT_HEADS * RT_DV
RT_U = 2 * RT_QK + 2 * RT_V
ROPE_BASE = 10000.0

D_FF = 2816

LANES = 128
SUBLANES = 8
PREFIX_ROWS = 2 * CHUNK
VMEM_LIMIT = 56 * 1024 * 1024

ROW_TILE = 512
MIX_TILE = 256
FF_CHUNK = 256
COL_CHUNK = 512


def _const_spec(shape):
    nd = len(shape)
    return pl.BlockSpec(shape, lambda *_: (0,) * nd, pipeline_mode=pl.Buffered(1))


def _params(n_axes):
    return pltpu.CompilerParams(dimension_semantics=("arbitrary",) * n_axes,
                                vmem_limit_bytes=VMEM_LIMIT)


def _rms(x, g):
    ms = jnp.mean(x * x, axis=-1, keepdims=True)
    return x * lax.rsqrt(ms + EPS) * g


def _inproj_kernel(*refs, n_out, has_gates):
    h_ref, g_ref, w_ref = refs[:3]
    pos = 3
    if has_gates:
        wg_ref = refs[pos]
        pos += 1
    o_ref = refs[pos]
    pos += 1
    if has_gates:
        og_ref = refs[pos]
        pos += 1
    hn_scr = refs[pos]

    hn_scr[...] = _rms(h_ref[...], g_ref[...]).astype(BF16)
    for c in range(n_out // COL_CHUNK):
        sl = slice(c * COL_CHUNK, (c + 1) * COL_CHUNK)
        o_ref[:, sl] = jnp.dot(hn_scr[...], w_ref[:, sl],
                               preferred_element_type=F32).astype(BF16)
    if has_gates:
        og_ref[...] = jnp.dot(hn_scr[...], wg_ref[...], preferred_element_type=F32)


def _inproj(h, g, w, wg=None):
    rows = h.shape[0]
    tm = min(ROW_TILE, rows)
    n_out = w.shape[1]
    has_gates = wg is not None
    in_specs = [pl.BlockSpec((tm, D_MODEL), lambda i: (i, 0)),
                _const_spec((1, D_MODEL)), _const_spec(w.shape)]
    args = [h, g, w]
    out_shape = [jax.ShapeDtypeStruct((rows, n_out), BF16)]
    out_specs = [pl.BlockSpec((tm, n_out), lambda i: (i, 0))]
    if has_gates:
        in_specs.append(_const_spec(wg.shape))
        args.append(wg)
        out_shape.append(jax.ShapeDtypeStruct((rows, LANES), F32))
        out_specs.append(pl.BlockSpec((tm, LANES), lambda i: (i, 0)))
    res = pl.pallas_call(
        functools.partial(_inproj_kernel, n_out=n_out, has_gates=has_gates),
        grid=(rows // tm,), in_specs=in_specs, out_specs=out_specs, out_shape=out_shape,
        scratch_shapes=[pltpu.VMEM((tm, D_MODEL), BF16)],
        compiler_params=_params(1), name="inproj_gates" if has_gates else "inproj")(*args)
    return res if has_gates else res[0]


def _post_kernel(*refs, final):
    h_ref, y_ref, wo_ref, gf_ref, wgu_ref, wd_ref = refs[:6]
    pos = 6
    if final:
        gfin_ref = refs[pos]
        pos += 1
    out_ref, h1_scr, hn_scr, a_scr = refs[pos:pos + 4]

    h1 = h_ref[...] + jnp.dot(y_ref[...], wo_ref[...], preferred_element_type=F32)
    h1_scr[...] = h1
    hn_scr[...] = _rms(h1, gf_ref[...]).astype(BF16)
    for c in range(D_FF // FF_CHUNK):
        lo = c * FF_CHUNK
        gate = jnp.dot(hn_scr[...], wgu_ref[:, lo:lo + FF_CHUNK], preferred_element_type=F32)
        up = jnp.dot(hn_scr[...], wgu_ref[:, D_FF + lo:D_FF + lo + FF_CHUNK],
                     preferred_element_type=F32)
        a_scr[:, lo:lo + FF_CHUNK] = (gate * jax.nn.sigmoid(gate) * up).astype(BF16)
    out = h1_scr[...] + jnp.dot(a_scr[...], wd_ref[...], preferred_element_type=F32)
    if final:
        out = _rms(out, gfin_ref[...])
    out_ref[...] = out


def _post(h, y, wo, gf, wgu, wd, gfin=None):
    rows = h.shape[0]
    tm = min(ROW_TILE, rows)
    final = gfin is not None
    dv = y.shape[1]
    in_specs = [pl.BlockSpec((tm, D_MODEL), lambda i: (i, 0)),
                pl.BlockSpec((tm, dv), lambda i: (i, 0)),
                _const_spec(wo.shape), _const_spec((1, D_MODEL)),
                _const_spec(wgu.shape), _const_spec(wd.shape)]
    args = [h, y, wo, gf, wgu, wd]
    if final:
        in_specs.append(_const_spec((1, D_MODEL)))
        args.append(gfin)
    return pl.pallas_call(
        functools.partial(_post_kernel, final=final),
        grid=(rows // tm,), in_specs=in_specs,
        out_specs=pl.BlockSpec((tm, D_MODEL), lambda i: (i, 0)),
        out_shape=jax.ShapeDtypeStruct((rows, D_MODEL), F32),
        scratch_shapes=[pltpu.VMEM((tm, D_MODEL), F32), pltpu.VMEM((tm, D_MODEL), BF16),
                        pltpu.VMEM((tm, D_FF), BF16)],
        compiler_params=_params(1), name="post_final" if final else "post")(*args)


def _lane_cumsum(x):
    lane = lax.broadcasted_iota(jnp.int32, x.shape, 1)
    sh = 1
    while sh < LANES:
        x = x + jnp.where(lane >= sh, pltpu.roll(x, sh, axis=1), 0.0)
        sh *= 2
    return x


def _to_col(row, t):
    return jnp.broadcast_to(row, (LANES, t)).T


def _mlstm_kernel(*refs, t, masked, emit_state):
    u_ref, gt_ref, wconv_ref, bg_ref, gn_ref, c0_ref, m0_ref, halo0_ref = refs[:8]
    pos = 8
    if masked:
        valid_ref = refs[pos]
        pos += 1
    y_ref = refs[pos]
    pos += 1
    if emit_state:
        c_out_ref, m_out_ref, halo_out_ref = refs[pos:pos + 3]
        pos += 3
    xbuf, cst, mst = refs[pos:pos + 3]
    nblk = t // LANES

    @pl.when(pl.program_id(1) == 0)
    def _():
        cst[...] = c0_ref[...]
        mst[...] = m0_ref[...]
        xbuf[0:SUBLANES, :] = halo0_ref[...]

    x = u_ref[:, 0:2 * ML_QK].astype(F32)
    if masked:
        x = x * valid_ref[...]
    xbuf[SUBLANES:SUBLANES + t, :] = x
    acc = x * wconv_ref[ML_CONV - 1:ML_CONV, :]
    for d in range(1, ML_CONV):
        acc = acc + xbuf[SUBLANES - d:SUBLANES - d + t, :] * wconv_ref[ML_CONV - 1 - d:ML_CONV - d, :]
    xbuf[0:SUBLANES, :] = xbuf[t:t + SUBLANES, :]
    qk = acc * jax.nn.sigmoid(acc)

    gp = gt_ref[...] + bg_ref[...]
    gc = GATE_CAP * jnp.tanh(gp / GATE_CAP)
    logf = -(jnp.maximum(-gc, 0.0) + jnp.log1p(jnp.exp(-jnp.abs(gc))))
    lane = lax.broadcasted_iota(jnp.int32, (t, LANES), 1)
    if masked:
        vmask = valid_ref[...] > 0.0
        gc = jnp.where(vmask, gc, -jnp.inf)
        logf = jnp.where(vmask, logf, 0.0)
    gl = jnp.where(lane < ML_HEADS, gc, logf)
    glt = gl.T
    li_rows = glt[0:ML_HEADS, :]
    lf_rows = glt[ML_HEADS:2 * ML_HEADS, :]
    blocks = []
    carry = None
    for b in range(nblk):
        cs = _lane_cumsum(lf_rows[:, b * LANES:(b + 1) * LANES])
        if carry is not None:
            cs = cs + carry
        carry = (carry if carry is not None else 0.0) + jnp.sum(
            lf_rows[:, b * LANES:(b + 1) * LANES], axis=1, keepdims=True)
        blocks.append(cs)
    g_rows = jnp.concatenate(blocks, axis=1) if nblk > 1 else blocks[0]
    g_end_all = carry

    row_id = lax.broadcasted_iota(jnp.int32, (t, LANES), 0)
    ones_blk = jnp.ones((t, LANES), BF16)
    lane_row = lax.broadcasted_iota(jnp.int32, (1, LANES), 1)

    for p in range(ML_HEADS // 2):
        qp = qk[:, p * LANES:(p + 1) * LANES] * (ML_DK ** -0.5)
        kp = qk[:, ML_QK + p * LANES:ML_QK + (p + 1) * LANES]
        kpb = kp.astype(BF16)
        cpair = cst[p]
        cpair_b = cpair.astype(BF16)
        for hl in range(2):
            h = 2 * p + hl
            hmask = (lane_row >= hl * ML_DK) & (lane_row < (hl + 1) * ML_DK)
            qh = jnp.where(hmask, qp, 0.0).astype(BF16)
            s_qk = lax.dot_general(qh, kpb, (((1,), (1,)), ((), ())),
                                   preferred_element_type=F32)
            g_row = g_rows[h:h + 1, :]
            li_row = li_rows[h:h + 1, :]
            g_col = _to_col(g_row, t)
            m_prev = mst[h:h + 1, :]
            inter = g_col + m_prev
            d_blocks = []
            dmax = None
            for b in range(nblk):
                sl = slice(b * LANES, (b + 1) * LANES)
                dd = li_row[:, sl] - jnp.abs(g_col - g_row[:, sl])
                allowed = ((lane + b * LANES) >> CHUNK_SHIFT) <= (row_id >> CHUNK_SHIFT)
                dd = jnp.where(allowed, dd, -jnp.inf)
                d_blocks.append(dd)
                dmax = dd if dmax is None else jnp.maximum(dmax, dd)
            m_new_col = jnp.maximum(inter, jnp.max(dmax, axis=1, keepdims=True))
            w_blocks = [jnp.exp(dd - m_new_col) for dd in d_blocks]
            w_full = jnp.concatenate(w_blocks, axis=1) if nblk > 1 else w_blocks[0]
            pmat = (s_qk * w_full).astype(BF16)
            vh = u_ref[:, 2 * ML_QK + h * ML_DV:2 * ML_QK + (h + 1) * ML_DV]
            v_aug = jnp.concatenate([vh, ones_blk], axis=1)
            tot = (jnp.exp(inter - m_new_col)[:, 0:1]
                   * jnp.dot(qh, cpair_b, preferred_element_type=F32)
                   + jnp.dot(pmat, v_aug, preferred_element_type=F32))
            num = tot[:, 0:ML_DV]
            den = tot[:, ML_DV:2 * ML_DV]
            hs = num / jnp.maximum(jnp.abs(den), jnp.exp(-m_new_col))
            hs = hs * lax.rsqrt(jnp.mean(hs * hs, axis=-1, keepdims=True) + EPS)
            og = u_ref[:, 2 * ML_QK + ML_V + h * ML_DV:2 * ML_QK + ML_V + (h + 1) * ML_DV].astype(F32)
            y_ref[:, h * ML_DV:(h + 1) * ML_DV] = (
                hs * gn_ref[:, h * ML_DV:(h + 1) * ML_DV] * jax.nn.sigmoid(og)).astype(BF16)

            g_end = g_end_all[h:h + 1, :]
            glog = li_row + g_end - g_row
            m_st = jnp.maximum(g_end + m_prev, jnp.max(glog, axis=1, keepdims=True))
            gk_col = _to_col(jnp.exp(glog - m_st[:, 0:1]), t)
            decay = jnp.exp(g_end + m_prev - m_st)
            kgt = (kp * gk_col).T.astype(BF16)
            upd = jnp.dot(kgt, v_aug, preferred_element_type=F32)
            rs = slice(hl * ML_DK, (hl + 1) * ML_DK)
            cst[p, rs, :] = decay[:, 0:1] * cpair[rs, :] + upd[rs, :]
            mst[h:h + 1, :] = m_st

    if emit_state:
        c_out_ref[...] = cst[...]
        m_out_ref[...] = mst[...]
        halo_out_ref[...] = xbuf[0:SUBLANES, :]


def _mlstm(u, gates, wconv, bgate, gnorm, c0, m0, halo0, *, n_seq, t, valid=None):
    rows = u.shape[0]
    steps = rows // (n_seq * t)
    masked = valid is not None
    emit_state = masked
    tile = lambda b, s: (b * steps + s, 0)
    in_specs = [pl.BlockSpec((t, ML_U), tile), pl.BlockSpec((t, LANES), tile),
                _const_spec(wconv.shape), _const_spec(bgate.shape), _const_spec(gnorm.shape),
                _const_spec(c0.shape), _const_spec(m0.shape), _const_spec(halo0.shape)]
    args = [u, gates, wconv, bgate, gnorm, c0, m0, halo0]
    if masked:
        in_specs.append(pl.BlockSpec((t, 1), tile))
        args.append(valid)
    out_shape = [jax.ShapeDtypeStruct((rows, ML_V), BF16)]
    out_specs = [pl.BlockSpec((t, ML_V), tile)]
    if emit_state:
        for a in (c0, m0, halo0):
            out_shape.append(jax.ShapeDtypeStruct(a.shape, F32))
            out_specs.append(pl.BlockSpec(a.shape, lambda b, s, nd=a.ndim: (0,) * nd))
    res = pl.pallas_call(
        functools.partial(_mlstm_kernel, t=t, masked=masked, emit_state=emit_state),
        grid=(n_seq, steps), in_specs=in_specs, out_specs=out_specs, out_shape=out_shape,
        scratch_shapes=[pltpu.VMEM((t + 2 * SUBLANES, 2 * ML_QK), F32),
                        pltpu.VMEM(c0.shape, F32), pltpu.VMEM(m0.shape, F32)],
        compiler_params=_params(2), name="mlstm_prefix" if masked else "mlstm")(*args)
    return res


def _ret_kernel(*refs, t, masked, emit_y):
    u_ref, cos_ref, sin_ref, mask_ref, qd_ref, kd_ref, cd_ref, gn_ref, s0_ref = refs[:9]
    pos = 9
    if masked:
        valid_ref = refs[pos]
        pos += 1
    if emit_y:
        y_ref = refs[pos]
        pos += 1
    else:
        s_out_ref = refs[pos]
        pos += 1
    sst = refs[pos]

    @pl.when(pl.program_id(1) == 0)
    def _():
        sst[...] = s0_ref[...]

    cos = cos_ref[...]
    sin = sin_ref[...]
    half = RT_DK // 2
    for h in range(RT_HEADS):
        q1 = u_ref[:, h * RT_DK:h * RT_DK + half].astype(F32)
        q2 = u_ref[:, h * RT_DK + half:(h + 1) * RT_DK].astype(F32)
        k1 = u_ref[:, RT_QK + h * RT_DK:RT_QK + h * RT_DK + half].astype(F32) * (RT_DK ** -0.5)
        k2 = u_ref[:, RT_QK + h * RT_DK + half:RT_QK + (h + 1) * RT_DK].astype(F32) * (RT_DK ** -0.5)
        qr = jnp.concatenate([q1 * cos - q2 * sin, q2 * cos + q1 * sin], axis=1)
        kr = jnp.concatenate([k1 * cos - k2 * sin, k2 * cos + k1 * sin], axis=1)
        if masked:
            kr = kr * valid_ref[...]
        qb = qr.astype(BF16)
        vh = u_ref[:, 2 * RT_QK + h * RT_DV:2 * RT_QK + (h + 1) * RT_DV]
        s_prev = sst[h]
        if emit_y:
            sc = lax.dot_general(qb, kr.astype(BF16), (((1,), (1,)), ((), ())),
                                 preferred_element_type=F32) * mask_ref[h]
            out = (jnp.dot(sc.astype(BF16), vh, preferred_element_type=F32)
                   + qd_ref[h] * jnp.dot(qb, s_prev.astype(BF16), preferred_element_type=F32))
            mu = jnp.mean(out, axis=-1, keepdims=True)
            cen = out - mu
            var = jnp.mean(cen * cen, axis=-1, keepdims=True)
            gate = u_ref[:, 2 * RT_QK + RT_V + h * RT_DV:2 * RT_QK + RT_V + (h + 1) * RT_DV].astype(F32)
            y_ref[:, h * RT_DV:(h + 1) * RT_DV] = (
                cen * lax.rsqrt(var + EPS) * gn_ref[:, h * RT_DV:(h + 1) * RT_DV]
                * (gate * jax.nn.sigmoid(gate))).astype(BF16)
        kdt = (kr * kd_ref[h]).T.astype(BF16)
        sst[h] = cd_ref[h] * s_prev + jnp.dot(kdt, vh, preferred_element_type=F32)

    if not emit_y:
        s_out_ref[...] = sst[...]


def _retention(u, cos, sin, mask, qd, kd, cd, gnorm, s0, *, n_seq, t, valid=None):
    rows = u.shape[0]
    steps = rows // (n_seq * t)
    masked = valid is not None
    emit_y = not masked
    tile = lambda b, s: (b * steps + s, 0)
    in_specs = [pl.BlockSpec((t, RT_U), tile),
                pl.BlockSpec((t, LANES), lambda b, s: (s, 0)),
                pl.BlockSpec((t, LANES), lambda b, s: (s, 0)),
                _const_spec(mask.shape), _const_spec(qd.shape), _const_spec(kd.shape),
                _const_spec(cd.shape), _const_spec(gnorm.shape), _const_spec(s0.shape)]
    args = [u, cos, sin, mask, qd, kd, cd, gnorm, s0]
    if masked:
        in_specs.append(pl.BlockSpec((t, 1), tile))
        args.append(valid)
    if emit_y:
        out_shape = jax.ShapeDtypeStruct((rows, RT_V), BF16)
        out_specs = pl.BlockSpec((t, RT_V), tile)
    else:
        out_shape = jax.ShapeDtypeStruct(s0.shape, F32)
        out_specs = pl.BlockSpec(s0.shape, lambda b, s: (0, 0, 0))
    return pl.pallas_call(
        functools.partial(_ret_kernel, t=t, masked=masked, emit_y=emit_y),
        grid=(n_seq, steps), in_specs=in_specs, out_specs=out_specs, out_shape=out_shape,
        scratch_shapes=[pltpu.VMEM(s0.shape, F32)],
        compiler_params=_params(2), name="retention_prefix" if masked else "retention")(*args)


def _ret_consts(t):
    log_gamma = jnp.log1p(-jnp.exp2(-5.0 - jnp.arange(RT_HEADS, dtype=F32)))
    idx = jnp.arange(t, dtype=F32)
    cid = jnp.arange(t) // CHUNK
    decay = jnp.exp(log_gamma[:, None, None] * jnp.abs(idx[:, None] - idx[None, :]))
    mask = jnp.where((cid[None, :] <= cid[:, None])[None], decay, 0.0)
    qd = jnp.exp(log_gamma[:, None] * (idx + 1.0))[:, :, None]
    kd = jnp.exp(log_gamma[:, None] * (t - 1.0 - idx))[:, :, None]
    cd = jnp.exp(log_gamma * t)[:, None, None]
    return mask, qd, kd, cd


def _rope_tables(pos):
    half = RT_DK // 2
    inv = ROPE_BASE ** (-jnp.arange(half, dtype=F32) / half)
    ang = pos.astype(F32)[:, None] * inv[None, :]
    return jnp.cos(ang), jnp.sin(ang)


def kernel(x, meta_tokens, norm_mix, norm_ffn, norm_final, ml_w_in, ml_b_gate, ml_w_conv,
           ml_norm, ml_w_out, rt_w_in, rt_norm, rt_w_out, ffn_w_gu, ffn_w_down):
    bsz, seq, _ = x.shape
    assert seq % MIX_TILE == 0 and (bsz * seq) % ROW_TILE == 0

    w_in0 = ml_w_in[0, :, :ML_U].astype(BF16)
    w_gate0 = jnp.pad(ml_w_in[0, :, ML_U:], ((0, 0), (0, LANES - 2 * ML_HEADS))).astype(BF16)
    b_gate0 = jnp.pad(ml_b_gate[0], (0, LANES - 2 * ML_HEADS))[None, :]
    w_out0 = ml_w_out[0].astype(BF16)
    w_in1 = rt_w_in[0].astype(BF16)
    w_out1 = rt_w_out[0].astype(BF16)
    w_gu = ffn_w_gu.astype(BF16)
    w_down = ffn_w_down.astype(BF16)
    g_mix = norm_mix[:, None, :]
    g_ffn = norm_ffn[:, None, :]

    n_fill = PREFIX_ROWS - N_META
    h_pre = jnp.concatenate([jnp.zeros((n_fill, D_MODEL), F32), meta_tokens.astype(F32)], axis=0)
    valid_pre = (jnp.arange(PREFIX_ROWS) >= n_fill).astype(F32)[:, None]
    h_main = x.reshape(bsz * seq, D_MODEL)

    c0 = jnp.zeros((ML_HEADS // 2, 2 * ML_DK, 2 * ML_DV), F32)
    m0 = jnp.zeros((ML_HEADS, LANES), F32)
    halo0 = jnp.zeros((SUBLANES, 2 * ML_QK), F32)
    gn0 = ml_norm[0][None, :]
    u_pre, gt_pre = _inproj(h_pre, g_mix[0], w_in0, w_gate0)
    y_pre, c1, m1, halo1 = _mlstm(u_pre, gt_pre, ml_w_conv[0], b_gate0, gn0, c0, m0, halo0,
                                  n_seq=1, t=PREFIX_ROWS, valid=valid_pre)
    h_pre = _post(h_pre, y_pre, w_out0, g_ffn[0], w_gu[0], w_down[0])

    u_main, gt_main = _inproj(h_main, g_mix[0], w_in0, w_gate0)
    (y_main,) = _mlstm(u_main, gt_main, ml_w_conv[0], b_gate0, gn0, c1, m1, halo1,
                       n_seq=bsz, t=MIX_TILE)
    h_main = _post(h_main, y_main, w_out0, g_ffn[0], w_gu[0], w_down[0])

    gn1 = rt_norm[0][None, :]
    s0 = jnp.zeros((RT_HEADS, RT_DK, RT_DV), F32)
    cos_p, sin_p = _rope_tables(jnp.arange(PREFIX_ROWS) - n_fill)
    u_pre = _inproj(h_pre, g_mix[1], w_in1)
    s1 = _retention(u_pre, cos_p, sin_p, *_ret_consts(PREFIX_ROWS), gn1, s0,
                    n_seq=1, t=PREFIX_ROWS, valid=valid_pre)

    cos_m, sin_m = _rope_tables(jnp.arange(seq) + N_META)
    u_main = _inproj(h_main, g_mix[1], w_in1)
    y_main = _retention(u_main, cos_m, sin_m, *_ret_consts(MIX_TILE), gn1, s1,
                        n_seq=bsz, t=MIX_TILE)
    out = _post(h_main, y_main, w_out1, g_ffn[1], w_gu[1], w_down[1], norm_final[None, :])
    return out.reshape(bsz, seq, D_MODEL)
```

```python
import functools

import jax
import jax.numpy as jnp
from jax import lax
from jax.experimental import pallas as pl
from jax.experimental.pallas import tpu as pltpu

F32 = jnp.float32
BF16 = jnp.bfloat16

D_MODEL = 1024
CHUNK = 64
CHUNK_SHIFT = 6
N_META = 16
N_PAD = CHUNK - N_META
EPS = 1e-6

ML_HEADS = 8
ML_DV = 128
ML_DK = 64
ML_QK = ML_HEADS * ML_DK
ML_V = ML_HEADS * ML_DV
ML_CONV = 4
GATE_CAP = 15.0
ML_U = 2 * ML_QK + 2 * ML_V

RT_HEADS = 4
RT_DK = 256
RT_DV = 512
RT_QK = RT_HEADS * RT_DK
RT_V = RT_HEADS * RT_DV
RT_U = 2 * RT_QK + 2 * RT_V
ROPE_BASE = 10000.0

D_FF = 2816

LANES = 128
SUBLANES = 8
PREFIX_ROWS = 2 * CHUNK
VMEM_LIMIT = 56 * 1024 * 1024

ROW_TILE = 512
MIX_TILE = 256
FF_CHUNK = 256
COL_CHUNK = 512


def _const_spec(shape):
    nd = len(shape)
    return pl.BlockSpec(shape, lambda *_: (0,) * nd, pipeline_mode=pl.Buffered(1))


def _layer_spec(stack, layer):
    nd = stack.ndim - 1
    return pl.BlockSpec((None,) + stack.shape[1:], lambda *_: (layer,) + (0,) * nd,
                        pipeline_mode=pl.Buffered(1))


def _params(n_axes):
    return pltpu.CompilerParams(dimension_semantics=("arbitrary",) * n_axes,
                                vmem_limit_bytes=VMEM_LIMIT)


def _rms(x, g):
    ms = jnp.mean(x * x, axis=-1, keepdims=True)
    return x * lax.rsqrt(ms + EPS) * g


def _inproj_kernel(*refs, n_out, has_gates):
    h_ref, g_ref, w_ref = refs[:3]
    pos = 3
    if has_gates:
        wg_ref = refs[pos]
        pos += 1
    o_ref = refs[pos]
    pos += 1
    if has_gates:
        og_ref = refs[pos]
        pos += 1
    hn_scr = refs[pos]

    hn_scr[...] = _rms(h_ref[...], g_ref[...]).astype(BF16)
    for c in range(n_out // COL_CHUNK):
        sl = slice(c * COL_CHUNK, (c + 1) * COL_CHUNK)
        o_ref[:, sl] = jnp.dot(hn_scr[...], w_ref[:, sl],
                               preferred_element_type=F32).astype(BF16)
    if has_gates:
        og_ref[...] = jnp.dot(hn_scr[...], wg_ref[...], preferred_element_type=F32)


def _inproj(h, g, g_idx, w, w_idx, n_out, wg=None):
    rows = h.shape[0]
    tm = min(ROW_TILE, rows)
    has_gates = wg is not None
    in_specs = [pl.BlockSpec((tm, D_MODEL), lambda i: (i, 0)),
                _layer_spec(g, g_idx), _layer_spec(w, w_idx)]
    args = [h, g, w]
    out_shape = [jax.ShapeDtypeStruct((rows, n_out), BF16)]
    out_specs = [pl.BlockSpec((tm, n_out), lambda i: (i, 0))]
    if has_gates:
        in_specs.append(_const_spec(wg.shape))
        args.append(wg)
        out_shape.append(jax.ShapeDtypeStruct((rows, LANES), F32))
        out_specs.append(pl.BlockSpec((tm, LANES), lambda i: (i, 0)))
    res = pl.pallas_call(
        functools.partial(_inproj_kernel, n_out=n_out, has_gates=has_gates),
        grid=(rows // tm,), in_specs=in_specs, out_specs=out_specs, out_shape=out_shape,
        scratch_shapes=[pltpu.VMEM((tm, D_MODEL), BF16)],
        compiler_params=_params(1), name="inproj_gates" if has_gates else "inproj")(*args)
    return res if has_gates else res[0]


def _post_kernel(*refs, final):
    h_ref, y_ref, wo_ref, gf_ref, wgu_ref, wd_ref = refs[:6]
    pos = 6
    if final:
        gfin_ref = refs[pos]
        pos += 1
    out_ref, h1_scr, hn_scr, a_scr = refs[pos:pos + 4]

    h1 = h_ref[...] + jnp.dot(y_ref[...], wo_ref[...], preferred_element_type=F32)
    h1_scr[...] = h1
    hn_scr[...] = _rms(h1, gf_ref[...]).astype(BF16)
    for c in range(D_FF // FF_CHUNK):
        lo = c * FF_CHUNK
        gate = jnp.dot(hn_scr[...], wgu_ref[:, lo:lo + FF_CHUNK], preferred_element_type=F32)
        up = jnp.dot(hn_scr[...], wgu_ref[:, D_FF + lo:D_FF + lo + FF_CHUNK],
                     preferred_element_type=F32)
        a_scr[:, lo:lo + FF_CHUNK] = (gate * jax.nn.sigmoid(gate) * up).astype(BF16)
    out = h1_scr[...] + jnp.dot(a_scr[...], wd_ref[...], preferred_element_type=F32)
    if final:
        out = _rms(out, gfin_ref[...])
    out_ref[...] = out


def _post(h, y, wo, gf, wgu, wd, layer, gfin=None):
    rows = h.shape[0]
    tm = min(ROW_TILE, rows)
    final = gfin is not None
    dv = y.shape[1]
    in_specs = [pl.BlockSpec((tm, D_MODEL), lambda i: (i, 0)),
                pl.BlockSpec((tm, dv), lambda i: (i, 0)),
                _layer_spec(wo, 0), _layer_spec(gf, layer),
                _layer_spec(wgu, layer), _layer_spec(wd, layer)]
    args = [h, y, wo, gf, wgu, wd]
    if final:
        in_specs.append(_const_spec((1, D_MODEL)))
        args.append(gfin)
    return pl.pallas_call(
        functools.partial(_post_kernel, final=final),
        grid=(rows // tm,), in_specs=in_specs,
        out_specs=pl.BlockSpec((tm, D_MODEL), lambda i: (i, 0)),
        out_shape=jax.ShapeDtypeStruct((rows, D_MODEL), F32),
        scratch_shapes=[pltpu.VMEM((tm, D_MODEL), F32), pltpu.VMEM((tm, D_MODEL), BF16),
                        pltpu.VMEM((tm, D_FF), BF16)],
        compiler_params=_params(1), name="post_final" if final else "post")(*args)


def _lane_cumsum(x):
    lane = lax.broadcasted_iota(jnp.int32, x.shape, 1)
    sh = 1
    while sh < LANES:
        x = x + jnp.where(lane >= sh, pltpu.roll(x, sh, axis=1), 0.0)
        sh *= 2
    return x


def _to_col(row, t):
    return jnp.broadcast_to(row, (LANES, t)).T


def _mlstm_kernel(*refs, t, masked, emit_state):
    u_ref, gt_ref, wconv_ref, bg_ref, gn_ref, c0_ref, halo0_ref = refs[:7]
    pos = 7
    if masked:
        valid_ref, valid_row_ref = refs[pos:pos + 2]
        pos += 2
    y_ref = refs[pos]
    pos += 1
    if emit_state:
        c_out_ref, halo_out_ref = refs[pos:pos + 2]
        pos += 2
    xbuf, cst = refs[pos:pos + 2]
    nblk = t // LANES

    @pl.when(pl.program_id(1) == 0)
    def _():
        cst[...] = c0_ref[...]
        xbuf[0:SUBLANES, :] = halo0_ref[...]

    x = u_ref[:, 0:2 * ML_QK].astype(F32)
    if masked:
        x = x * valid_ref[...]
    xbuf[SUBLANES:SUBLANES + t, :] = x
    acc = x * wconv_ref[ML_CONV - 1:ML_CONV, :]
    for d in range(1, ML_CONV):
        acc = acc + xbuf[SUBLANES - d:SUBLANES - d + t, :] * wconv_ref[ML_CONV - 1 - d:ML_CONV - d, :]
    xbuf[0:SUBLANES, :] = xbuf[t:t + SUBLANES, :]
    qk = acc * jax.nn.sigmoid(acc)

    gp = gt_ref[...].T[0:2 * ML_HEADS, :] + bg_ref[...]
    gc = GATE_CAP * jnp.tanh(gp / GATE_CAP)
    li_rows = gc[0:ML_HEADS, :]
    gf = gc[ML_HEADS:2 * ML_HEADS, :]
    lf_rows = -(jnp.maximum(-gf, 0.0) + jnp.log1p(jnp.exp(-jnp.abs(gf))))
    if masked:
        vrow = valid_row_ref[...] > 0.0
        li_rows = jnp.where(vrow, li_rows, -jnp.inf)
        lf_rows = jnp.where(vrow, lf_rows, 0.0)
    blocks = []
    carry = None
    for b in range(nblk):
        blk = lf_rows[:, b * LANES:(b + 1) * LANES]
        cs = _lane_cumsum(blk)
        tot_b = jnp.sum(blk, axis=1, keepdims=True)
        if carry is not None:
            cs = cs + carry
            carry = carry + tot_b
        else:
            carry = tot_b
        blocks.append(cs)
    g_rows = jnp.concatenate(blocks, axis=1) if nblk > 1 else blocks[0]
    g_end_all = carry
    gk_rows = jnp.exp(li_rows + g_end_all - g_rows)
    decay_all = jnp.exp(g_end_all)

    sub_id = lax.broadcasted_iota(jnp.int32, (LANES, LANES), 0)
    lane_id = lax.broadcasted_iota(jnp.int32, (LANES, LANES), 1)
    same_or_earlier_chunk = (lane_id >> CHUNK_SHIFT) <= (sub_id >> CHUNK_SHIFT)
    ones_blk = jnp.ones((t, LANES), BF16)
    zero_blk = jnp.zeros((LANES, LANES), BF16)
    lane_row = lax.broadcasted_iota(jnp.int32, (1, LANES), 1)

    for p in range(ML_HEADS // 2):
        qp = qk[:, p * LANES:(p + 1) * LANES] * (ML_DK ** -0.5)
        kp = qk[:, ML_QK + p * LANES:ML_QK + (p + 1) * LANES]
        kpb = kp.astype(BF16)
        cpair = cst[p]
        cpair_b = cpair.astype(BF16)
        for hl in range(2):
            h = 2 * p + hl
            hmask = (lane_row >= hl * ML_DK) & (lane_row < (hl + 1) * ML_DK)
            qh = jnp.where(hmask, qp, 0.0)
            qhb = qh.astype(BF16)
            s_qk = lax.dot_general(qhb, kpb, (((1,), (1,)), ((), ())),
                                   preferred_element_type=F32)
            g_row = g_rows[h:h + 1, :]
            li_row = li_rows[h:h + 1, :]
            g_col = _to_col(g_row, t)
            p_rows = []
            for rb in range(nblk):
                rsl = slice(rb * LANES, (rb + 1) * LANES)
                p_blocks = []
                for cb in range(nblk):
                    if cb > rb:
                        p_blocks.append(zero_blk)
                        continue
                    csl = slice(cb * LANES, (cb + 1) * LANES)
                    dd = li_row[:, csl] - jnp.abs(g_col[rsl, :] - g_row[:, csl])
                    if cb == rb:
                        dd = jnp.where(same_or_earlier_chunk, dd, -jnp.inf)
                    p_blocks.append((s_qk[rsl, csl] * jnp.exp(dd)).astype(BF16))
                p_rows.append(jnp.concatenate(p_blocks, axis=1) if nblk > 1 else p_blocks[0])
            pmat = jnp.concatenate(p_rows, axis=0) if nblk > 1 else p_rows[0]
            vh = u_ref[:, 2 * ML_QK + h * ML_DV:2 * ML_QK + (h + 1) * ML_DV]
            v_aug = jnp.concatenate([vh, ones_blk], axis=1)
            lhs = jnp.concatenate([(qh * jnp.exp(g_col)).astype(BF16), pmat], axis=1)
            rhs = jnp.concatenate([cpair_b, v_aug], axis=0)
            tot = jnp.dot(lhs, rhs, preferred_element_type=F32)
            num = tot[:, 0:ML_DV]
            den = tot[:, ML_DV:2 * ML_DV]
            hs = num / jnp.maximum(jnp.abs(den), 1.0)
            hs = hs * lax.rsqrt(jnp.mean(hs * hs, axis=-1, keepdims=True) + EPS)
            og = u_ref[:, 2 * ML_QK + ML_V + h * ML_DV:2 * ML_QK + ML_V + (h + 1) * ML_DV].astype(F32)
            y_ref[:, h * ML_DV:(h + 1) * ML_DV] = (
                hs * gn_ref[:, h * ML_DV:(h + 1) * ML_DV] * jax.nn.sigmoid(og)).astype(BF16)

            gk_col = _to_col(gk_rows[h:h + 1, :], t)
            kgt = (kp * gk_col).T.astype(BF16)
            upd = jnp.dot(kgt, v_aug, preferred_element_type=F32)
            rs = slice(hl * ML_DK, (hl + 1) * ML_DK)
            cst[p, rs, :] = decay_all[h:h + 1, :] * cpair[rs, :] + upd[rs, :]

    if emit_state:
        c_out_ref[...] = cst[...]
        halo_out_ref[...] = xbuf[0:SUBLANES, :]


def _mlstm(u, gates, wconv, bgate, gnorm, c0, halo0, *, n_seq, t, valid=None):
    rows = u.shape[0]
    steps = rows // (n_seq * t)
    masked = valid is not None
    emit_state = masked
    tile = lambda b, s: (b * steps + s, 0)
    in_specs = [pl.BlockSpec((t, ML_U), tile), pl.BlockSpec((t, LANES), tile),
                _const_spec(wconv.shape), _const_spec(bgate.shape), _layer_spec(gnorm, 0),
                _const_spec(c0.shape), _const_spec(halo0.shape)]
    args = [u, gates, wconv, bgate, gnorm, c0, halo0]
    if masked:
        in_specs += [pl.BlockSpec((t, 1), tile), _const_spec((1, t))]
        args += [valid, valid.reshape(1, t)]
    out_shape = [jax.ShapeDtypeStruct((rows, ML_V), BF16)]
    out_specs = [pl.BlockSpec((t, ML_V), tile)]
    if emit_state:
        for a in (c0, halo0):
            out_shape.append(jax.ShapeDtypeStruct(a.shape, F32))
            out_specs.append(pl.BlockSpec(a.shape, lambda b, s, nd=a.ndim: (0,) * nd))
    res = pl.pallas_call(
        functools.partial(_mlstm_kernel, t=t, masked=masked, emit_state=emit_state),
        grid=(n_seq, steps), in_specs=in_specs, out_specs=out_specs, out_shape=out_shape,
        scratch_shapes=[pltpu.VMEM((t + 2 * SUBLANES, 2 * ML_QK), F32),
                        pltpu.VMEM(c0.shape, F32)],
        compiler_params=_params(2), name="mlstm_prefix" if masked else "mlstm")(*args)
    return res


def _ret_kernel(*refs, t, masked, emit_y):
    (u_ref, ca_ref, sa_ref, cb_ref, sb_ref, mask_ref, qd_ref, kd_ref, cd_ref, gn_ref,
     s0_ref) = refs[:11]
    pos = 11
    if masked:
        valid_ref = refs[pos]
        pos += 1
    if emit_y:
        y_ref = refs[pos]
        pos += 1
    else:
        s_out_ref = refs[pos]
        pos += 1
    sst = refs[pos]

    @pl.when(pl.program_id(1) == 0)
    def _():
        sst[...] = s0_ref[...]

    cos = cb_ref[...] * ca_ref[...] - sb_ref[...] * sa_ref[...]
    sin = sb_ref[...] * ca_ref[...] + cb_ref[...] * sa_ref[...]
    half = RT_DK // 2
    for h in range(RT_HEADS):
        q1 = u_ref[:, h * RT_DK:h * RT_DK + half].astype(F32)
        q2 = u_ref[:, h * RT_DK + half:(h + 1) * RT_DK].astype(F32)
        k1 = u_ref[:, RT_QK + h * RT_DK:RT_QK + h * RT_DK + half].astype(F32)
        k2 = u_ref[:, RT_QK + h * RT_DK + half:RT_QK + (h + 1) * RT_DK].astype(F32)
        qr = jnp.concatenate([q1 * cos - q2 * sin, q2 * cos + q1 * sin], axis=1)
        kr = jnp.concatenate([k1 * cos - k2 * sin, k2 * cos + k1 * sin], axis=1)
        if masked:
            kr = kr * valid_ref[...]
        qb = qr.astype(BF16)
        vh = u_ref[:, 2 * RT_QK + h * RT_DV:2 * RT_QK + (h + 1) * RT_DV]
        s_prev = sst[h]
        if emit_y:
            sc = lax.dot_general(qb, kr.astype(BF16), (((1,), (1,)), ((), ())),
                                 preferred_element_type=F32) * mask_ref[h]
            out = (jnp.dot(sc.astype(BF16), vh, preferred_element_type=F32)
                   + qd_ref[h] * jnp.dot(qb, s_prev.astype(BF16), preferred_element_type=F32))
            mu = jnp.mean(out, axis=-1, keepdims=True)
            cen = out - mu
            var = jnp.mean(cen * cen, axis=-1, keepdims=True)
            gate = u_ref[:, 2 * RT_QK + RT_V + h * RT_DV:2 * RT_QK + RT_V + (h + 1) * RT_DV].astype(F32)
            y_ref[:, h * RT_DV:(h + 1) * RT_DV] = (
                cen * lax.rsqrt(var + EPS) * gn_ref[:, h * RT_DV:(h + 1) * RT_DV]
                * (gate * jax.nn.sigmoid(gate))).astype(BF16)
        kdt = (kr * kd_ref[h]).T.astype(BF16)
        sst[h] = cd_ref[h] * s_prev + jnp.dot(kdt, vh, preferred_element_type=F32)

    if not emit_y:
        s_out_ref[...] = sst[...]


def _retention(u, rope, consts, gnorm, s0, *, n_seq, t, valid=None):
    rows = u.shape[0]
    steps = rows // (n_seq * t)
    masked = valid is not None
    emit_y = not masked
    tile = lambda b, s: (b * steps + s, 0)
    cos_a, sin_a, cos_b, sin_b = rope
    base = pl.BlockSpec((None, 1, LANES), lambda b, s: (s, 0, 0))
    in_specs = [pl.BlockSpec((t, RT_U), tile), base, base,
                _const_spec(cos_b.shape), _const_spec(sin_b.shape)]
    in_specs += [_const_spec(c.shape) for c in consts]
    in_specs += [_layer_spec(gnorm, 0), _const_spec(s0.shape)]
    args = [u, cos_a, sin_a, cos_b, sin_b, *consts, gnorm, s0]
    if masked:
        in_specs.append(pl.BlockSpec((t, 1), tile))
        args.append(valid)
    if emit_y:
        out_shape = jax.ShapeDtypeStruct((rows, RT_V), BF16)
        out_specs = pl.BlockSpec((t, RT_V), tile)
    else:
        out_shape = jax.ShapeDtypeStruct(s0.shape, F32)
        out_specs = pl.BlockSpec(s0.shape, lambda b, s: (0, 0, 0))
    return pl.pallas_call(
        functools.partial(_ret_kernel, t=t, masked=masked, emit_y=emit_y),
        grid=(n_seq, steps), in_specs=in_specs, out_specs=out_specs, out_shape=out_shape,
        scratch_shapes=[pltpu.VMEM(s0.shape, F32)],
        compiler_params=_params(2), name="retention_prefix" if masked else "retention")(*args)


def _ret_consts(t):
    scale = RT_DK ** -0.5
    log_gamma = jnp.log1p(-jnp.exp2(-5.0 - jnp.arange(RT_HEADS, dtype=F32)))
    idx = jnp.arange(t, dtype=F32)
    cid = jnp.arange(t) // CHUNK
    decay = jnp.exp(log_gamma[:, None, None] * jnp.abs(idx[:, None] - idx[None, :]))
    mask = jnp.where((cid[None, :] <= cid[:, None])[None], decay, 0.0) * scale
    qd = jnp.exp(log_gamma[:, None] * (idx + 1.0))[:, :, None]
    kd = jnp.exp(log_gamma[:, None] * (t - 1.0 - idx))[:, :, None] * scale
    cd = jnp.exp(log_gamma * t)[:, None, None]
    return mask, qd, kd, cd


def _rope_tables(first_pos, steps, t):
    half = RT_DK // 2
    inv = ROPE_BASE ** (-jnp.arange(half, dtype=F32) / half)
    base = (first_pos + t * jnp.arange(steps)).astype(F32)[:, None, None] * inv[None, None, :]
    off = jnp.arange(t, dtype=F32)[:, None] * inv[None, :]
    return jnp.cos(base), jnp.sin(base), jnp.cos(off), jnp.sin(off)


def kernel(x, meta_tokens, norm_mix, norm_ffn, norm_final, ml_w_in, ml_b_gate, ml_w_conv,
           ml_norm, ml_w_out, rt_w_in, rt_norm, rt_w_out, ffn_w_gu, ffn_w_down):
    bsz, seq, _ = x.shape
    assert seq % MIX_TILE == 0 and (bsz * seq) % ROW_TILE == 0

    w_in0 = ml_w_in.astype(BF16)
    w_gate0 = jnp.pad(ml_w_in[0, :, ML_U:], ((0, 0), (0, LANES - 2 * ML_HEADS))).astype(BF16)
    b_gate0 = ml_b_gate[0][:, None]
    w_out0 = ml_w_out.astype(BF16)
    w_in1 = rt_w_in.astype(BF16)
    w_out1 = rt_w_out.astype(BF16)
    w_gu = ffn_w_gu.astype(BF16)
    w_down = ffn_w_down.astype(BF16)
    g_mix = norm_mix[:, None, :]
    g_ffn = norm_ffn[:, None, :]
    gn0 = ml_norm[:, None, :]
    gn1 = rt_norm[:, None, :]

    n_fill = PREFIX_ROWS - N_META
    h_pre = jnp.concatenate([jnp.zeros((n_fill, D_MODEL), F32), meta_tokens.astype(F32)], axis=0)
    valid_pre = (jnp.arange(PREFIX_ROWS) >= n_fill).astype(F32)[:, None]
    h_main = x.reshape(bsz * seq, D_MODEL)

    c0 = jnp.zeros((ML_HEADS // 2, 2 * ML_DK, 2 * ML_DV), F32)
    halo0 = jnp.zeros((SUBLANES, 2 * ML_QK), F32)
    u_pre, gt_pre = _inproj(h_pre, g_mix, 0, w_in0, 0, ML_U, w_gate0)
    y_pre, c1, halo1 = _mlstm(u_pre, gt_pre, ml_w_conv[0], b_gate0, gn0, c0, halo0,
                              n_seq=1, t=PREFIX_ROWS, valid=valid_pre)
    h_pre = _post(h_pre, y_pre, w_out0, g_ffn, w_gu, w_down, 0)

    u_main, gt_main = _inproj(h_main, g_mix, 0, w_in0, 0, ML_U, w_gate0)
    (y_main,) = _mlstm(u_main, gt_main, ml_w_conv[0], b_gate0, gn0, c1, halo1,
                       n_seq=bsz, t=MIX_TILE)
    h_main = _post(h_main, y_main, w_out0, g_ffn, w_gu, w_down, 0)

    s0 = jnp.zeros((RT_HEADS, RT_DK, RT_DV), F32)
    u_pre = _inproj(h_pre, g_mix, 1, w_in1, 0, RT_U)
    s1 = _retention(u_pre, _rope_tables(-n_fill, 1, PREFIX_ROWS), _ret_consts(PREFIX_ROWS),
                    gn1, s0, n_seq=1, t=PREFIX_ROWS, valid=valid_pre)

    u_main = _inproj(h_main, g_mix, 1, w_in1, 0, RT_U)
    y_main = _retention(u_main, _rope_tables(N_META, seq // MIX_TILE, MIX_TILE),
                        _ret_consts(MIX_TILE), gn1, s1, n_seq=bsz, t=MIX_TILE)
    out = _post(h_main, y_main, w_out1, g_ffn, w_gu, w_down, 1, norm_final[None, :])
    return out.reshape(bsz, seq, D_MODEL)
```
